```python
import math
import jax, jax.numpy as jnp
from jax import lax
import numpy as np

D_MODEL = 1024
BATCH = 8
SEQ = 4096
DEPTH = 2

D_MIX = D_MODEL
D_FOURIER = D_MIX // 4
N_FOURIER_HEADS = 4
FOURIER_HEAD_DIM = D_FOURIER // N_FOURIER_HEADS
D_DIFF = D_MIX // 2
N_DIFF_HEADS = 4
DIFF_HEAD_DIM = D_DIFF // (2 * N_DIFF_HEADS)
DIFF_V_DIM = 2 * DIFF_HEAD_DIM
D_POOL = D_MIX - D_FOURIER - D_DIFF
POOL_WINDOWS = (2, 4, 8, 16)
N_POOL_GROUPS = len(POOL_WINDOWS)
POOL_GROUP_DIM = D_POOL // N_POOL_GROUPS
D_IN_PROJ = D_FOURIER + 3 * D_DIFF + D_POOL
D_FF = ((8 * D_MODEL // 3 + 127) // 128) * 128
N_SUBLAYERS = 3
ROPE_THETA = 10000.0
Q_BLOCK = 128
NORM_EPS = 1e-6
SUBLN_EPS = 1e-5
MACARON_WEIGHT = 0.5

kernel_name = "hybrid_fourier_diffattn_pool_macaron_encoder"


def rms_norm(x, g, eps=NORM_EPS):
    xf = x.astype(jnp.float32)
    y = xf * lax.rsqrt(jnp.mean(xf * xf, axis=-1, keepdims=True) + eps)
    return (y * g.astype(jnp.float32)).astype(x.dtype)


def modulate(h, shift, scale):
    return h * (1 + scale[:, None, :]) + shift[:, None, :]


def rope_tables(positions):
    inv = 1.0 / (ROPE_THETA ** (jnp.arange(0, DIFF_HEAD_DIM, 2, dtype=jnp.float32) / DIFF_HEAD_DIM))
    ang = positions.astype(jnp.float32)[:, None] * inv[None, :]
    ang = jnp.concatenate([ang, ang], axis=-1)
    return jnp.cos(ang), jnp.sin(ang)


def apply_rope(t, cos, sin):
    t1, t2 = jnp.split(t, 2, axis=-1)
    rot = jnp.concatenate([-t2, t1], axis=-1)
    return (t * cos[None, :, None, :] + rot * sin[None, :, None, :]).astype(t.dtype)


def swiglu(h, w_gu, w_down):
    g, u = jnp.split(h @ w_gu, 2, axis=-1)
    return (jax.nn.silu(g) * u) @ w_down


def fourier_mixer(u, w_f):
    B, S, _ = u.shape
    uh = u.reshape(B, S, N_FOURIER_HEADS, FOURIER_HEAD_DIM).astype(jnp.float32)
    f = jnp.fft.fft2(uh, axes=(1, 3), norm="ortho").real
    return f.reshape(B, S, D_FOURIER).astype(u.dtype) @ w_f


def diff_attention(q, k, v, lam, lambda_init, g_subln, cos, sin):
    B, S, _ = q.shape
    H, d = N_DIFF_HEADS, DIFF_HEAD_DIM
    q = apply_rope(q.reshape(B, S, 2 * H, d), cos, sin)
    k = apply_rope(k.reshape(B, S, 2 * H, d), cos, sin)
    q = q.reshape(B, S, H, 2, d).transpose(0, 2, 3, 1, 4)
    k = k.reshape(B, S, H, 2, d).transpose(0, 2, 3, 1, 4)
    v = v.reshape(B, S, H, DIFF_V_DIM).transpose(0, 2, 1, 3)
    nblk = S // Q_BLOCK
    qb = q.reshape(B, H, 2, nblk, Q_BLOCK, d).transpose(3, 0, 1, 2, 4, 5)
    scale = d ** -0.5

    def block(q_blk):
        s = jnp.einsum('bhmqd,bhmkd->bhmqk', q_blk, k).astype(jnp.float32) * scale
        p = jax.nn.softmax(s, axis=-1)
        a = p[:, :, 0] - lam * p[:, :, 1]
        return jnp.einsum('bhqk,bhkv->bhqv', a.astype(v.dtype), v)

    o = lax.map(block, qb)
    o = o.transpose(1, 0, 3, 2, 4).reshape(B, S, H, DIFF_V_DIM)
    o = rms_norm(o, g_subln, SUBLN_EPS) * (1.0 - lambda_init)
    return o.reshape(B, S, D_DIFF)


def pool_mixer(u, w_pool, pool_scale):
    B, S, _ = u.shape
    uf = u.astype(jnp.float32)
    pos = jnp.arange(S)
    outs = []
    for g, w in enumerate(POOL_WINDOWS):
        xg = uf[..., g * POOL_GROUP_DIM:(g + 1) * POOL_GROUP_DIM]
        half = w // 2
        xp = jnp.pad(xg, ((0, 0), (half + 1, half), (0, 0)))
        cs = jnp.cumsum(xp, axis=1)
        win_sum = cs[:, w:w + S] - cs[:, 0:S]
        count = (jnp.minimum(pos + half, S) - jnp.maximum(pos - half, 0)).astype(jnp.float32)
        outs.append(win_sum / count[None, :, None] - xg)
    pooled = jnp.stack(outs, axis=2)
    y = jnp.einsum('bsgc,gcd->bsgd', pooled.astype(u.dtype), w_pool).reshape(B, S, D_POOL)
    return y * pool_scale


def setup_inputs(seed: int = 0) -> dict:
    key = jax.random.key(seed)
    ks = jax.random.split(key, 20)
    L, D = DEPTH, D_MODEL

    def dense(k, shape, fan_in):
        return jax.random.normal(k, shape, jnp.float32) * (fan_in ** -0.5)

    def noise(k, shape, s):
        return s * jax.random.normal(k, shape, jnp.float32)

    return {
        "x": jax.random.normal(ks[0], (BATCH, SEQ, D), jnp.float32),
        "c": jax.random.normal(ks[1], (BATCH, D), jnp.float32),
        "positions": jnp.arange(SEQ, dtype=jnp.int32),
        "w_ada": dense(ks[2], (L, D, N_SUBLAYERS * 3 * D), D),
        "b_ada": noise(ks[3], (L, N_SUBLAYERS * 3 * D), 0.01),
        "g_pre": 1.0 + noise(ks[4], (L, N_SUBLAYERS, D), 0.02),
        "g_post": 1.0 + noise(ks[5], (L, N_SUBLAYERS, D), 0.02),
        "w_ff_gu": dense(ks[6], (L, 2, D, 2 * D_FF), D),
        "w_ff_down": dense(ks[7], (L, 2, D_FF, D), D_FF),
        "w_in": dense(ks[8], (L, D, D_IN_PROJ), D),
        "w_fourier": dense(ks[9], (L, D_FOURIER, D_FOURIER), D_FOURIER),
        "lambda_q1": noise(ks[10], (L, DIFF_HEAD_DIM), 0.1),
        "lambda_k1": noise(ks[11], (L, DIFF_HEAD_DIM), 0.1),
        "lambda_q2": noise(ks[12], (L, DIFF_HEAD_DIM), 0.1),
        "lambda_k2": noise(ks[13], (L, DIFF_HEAD_DIM), 0.1),
        "g_subln": 1.0 + noise(ks[14], (L, DIFF_V_DIM), 0.02),
        "w_pool": dense(ks[15], (L, N_POOL_GROUPS, POOL_GROUP_DIM, POOL_GROUP_DIM), POOL_GROUP_DIM),
        "pool_scale": 1.0 + noise(ks[16], (L, D_POOL), 0.1),
        "w_out": dense(ks[17], (L, D_MIX, D), D_MIX),
    }


def reference(x, c, positions, w_ada, b_ada, g_pre, g_post, w_ff_gu, w_ff_down, w_in,
              w_fourier, lambda_q1, lambda_k1, lambda_q2, lambda_k2, g_subln, w_pool,
              pool_scale, w_out):
    B = x.shape[0]
    D = D_MODEL
    cos, sin = rope_tables(positions)
    c_act = jax.nn.silu(c)
    splits = [D_FOURIER, D_FOURIER + D_DIFF, D_FOURIER + 2 * D_DIFF, D_FOURIER + 3 * D_DIFF]
    for l in range(DEPTH):
        ada = (c_act @ w_ada[l] + b_ada[l]).reshape(B, N_SUBLAYERS, 3, D)
        shift, scale, gate = ada[:, :, 0], ada[:, :, 1], ada[:, :, 2]

        h = modulate(rms_norm(x, g_pre[l, 0]), shift[:, 0], scale[:, 0])
        y = swiglu(h, w_ff_gu[l, 0], w_ff_down[l, 0])
        x = x + MACARON_WEIGHT * gate[:, 0][:, None, :] * rms_norm(y, g_post[l, 0])

        h = modulate(rms_norm(x, g_pre[l, 1]), shift[:, 1], scale[:, 1])
        proj = h @ w_in[l]
        u_f, q, k, v, u_p = jnp.split(proj, splits, axis=-1)
        lambda_init = 0.8 - 0.6 * math.exp(-0.3 * l)
        lam = (jnp.exp(jnp.sum(lambda_q1[l].astype(jnp.float32) * lambda_k1[l].astype(jnp.float32)))
               - jnp.exp(jnp.sum(lambda_q2[l].astype(jnp.float32) * lambda_k2[l].astype(jnp.float32)))
               + lambda_init)
        y_f = fourier_mixer(u_f, w_fourier[l])
        y_d = diff_attention(q, k, v, lam, lambda_init, g_subln[l], cos, sin)
        y_p = pool_mixer(u_p, w_pool[l], pool_scale[l])
        y = jnp.concatenate([y_f, y_d, y_p], axis=-1) @ w_out[l]
        x = x + gate[:, 1][:, None, :] * rms_norm(y, g_post[l, 1])

        h = modulate(rms_norm(x, g_pre[l, 2]), shift[:, 2], scale[:, 2])
        y = swiglu(h, w_ff_gu[l, 1], w_ff_down[l, 1])
        x = x + MACARON_WEIGHT * gate[:, 2][:, None, :] * rms_norm(y, g_post[l, 2])
    return x
```

```python
import functools
import math

import jax
import jax.numpy as jnp
from jax import lax
from jax.experimental import pallas as pl
from jax.experimental.pallas import tpu as pltpu

N_FOURIER_HEADS = 4
N_DIFF_HEADS = 4
POOL_WINDOWS = (2, 4, 8, 16)
N_SUBLAYERS = 3
ROPE_THETA = 10000.0
NORM_EPS = 1e-6
SUBLN_EPS = 1e-5
MACARON_WEIGHT = 0.5

LANES = 128
SUBLANES = 8
MXU_DEPTH = 256
VMEM_BYTES_V7X = 64 * 1024 * 1024
VMEM_LIMIT = VMEM_BYTES_V7X - 8 * 1024 * 1024

TOKEN_TILE = 512
Q_TILE = 256
DFT_ROW_TILE = 512
FF_CHUNKS_IN_MXU_TILES = (4, 4, 3)

BF16 = jnp.bfloat16
F32 = jnp.float32


def _rms(v, eps):
    return v * lax.rsqrt(jnp.mean(v * v, axis=-1, keepdims=True) + eps)


def _resident(shape):
    nd = len(shape)
    return pl.BlockSpec(shape, lambda *_: (0,) * nd, pipeline_mode=pl.Buffered(1))


def _params(n_axes):
    return pltpu.CompilerParams(dimension_semantics=("arbitrary",) * n_axes, vmem_limit_bytes=VMEM_LIMIT)


def _ada_kernel(c_ref, w_ref, b_ref, o_ref):
    c = c_ref[...]
    c_act = (c * jax.nn.sigmoid(c)).astype(BF16)
    o_ref[0] = jnp.dot(c_act, w_ref[0].astype(BF16), preferred_element_type=F32) + b_ref[0]


def _ada(c, w_ada, b_ada):
    n_layers, d, n = w_ada.shape
    b = c.shape[0]
    tn = n // 8
    return pl.pallas_call(
        _ada_kernel,
        out_shape=jax.ShapeDtypeStruct((n_layers, b, n), F32),
        grid=(n_layers, n // tn),
        in_specs=[
            pl.BlockSpec((b, d), lambda l, j: (0, 0)),
            pl.BlockSpec((1, d, tn), lambda l, j: (l, 0, j)),
            pl.BlockSpec((1, 1, tn), lambda l, j: (l, 0, j)),
        ],
        out_specs=pl.BlockSpec((1, b, tn), lambda l, j: (l, 0, j)),
        compiler_params=_params(2),
        name="ada",
    )(c, w_ada, b_ada.reshape(n_layers, 1, n))


def _ffn_kernel(x_ref, mod_ref, gpre_ref, gpost_ref, wgu_ref, wd_ref, o_ref, *, d_ff, chunks):
    x = x_ref[...]
    mod = mod_ref[0]
    h = _rms(x, NORM_EPS) * gpre_ref[...] * (1.0 + mod[1:2]) + mod[0:1]
    hb = h.astype(BF16)
    y = jnp.zeros(x.shape, F32)
    for c0, c1 in chunks:
        g = jnp.dot(hb, wgu_ref[:, c0:c1], preferred_element_type=F32)
        u = jnp.dot(hb, wgu_ref[:, d_ff + c0:d_ff + c1], preferred_element_type=F32)
        a = (g * jax.nn.sigmoid(g) * u).astype(BF16)
        y = y + jnp.dot(a, wd_ref[c0:c1, :], preferred_element_type=F32)
    o_ref[...] = x + MACARON_WEIGHT * mod[2:3] * (_rms(y, NORM_EPS) * gpost_ref[...])


def _ffn(x2, mod, g_pre, g_post, w_gu, w_down, seq):
    t, d = x2.shape
    d_ff = w_down.shape[0]
    tm = TOKEN_TILE
    tiles_per_seq = seq // tm
    bounds, c0 = [], 0
    for n_tiles in FF_CHUNKS_IN_MXU_TILES:
        bounds.append((c0, c0 + n_tiles * MXU_DEPTH))
        c0 += n_tiles * MXU_DEPTH
    assert c0 == d_ff, (c0, d_ff)
    return pl.pallas_call(
        functools.partial(_ffn_kernel, d_ff=d_ff, chunks=tuple(bounds)),
        out_shape=jax.ShapeDtypeStruct((t, d), F32),
        grid=(t // tm,),
        in_specs=[
            pl.BlockSpec((tm, d), lambda i: (i, 0)),
            pl.BlockSpec((1, 3, d), lambda i: (i // tiles_per_seq, 0, 0)),
            _resident((1, d)),
            _resident((1, d)),
            _resident(w_gu.shape),
            _resident(w_down.shape),
        ],
        out_specs=pl.BlockSpec((tm, d), lambda i: (i, 0)),
        compiler_params=_params(1),
        name="ffn",
    )(x2, mod, g_pre, g_post, w_gu, w_down)


def _in_proj_kernel(x_ref, mod_ref, gpre_ref, win_ref, cos_ref, sin_ref, dftc_ref,
                    ab_ref, q_ref, k_ref, v_ref, up_ref, *, d_fourier, d_diff, head_dim):
    x = x_ref[...]
    mod = mod_ref[0]
    h = _rms(x, NORM_EPS) * gpre_ref[...] * (1.0 + mod[1:2]) + mod[0:1]
    proj = jnp.dot(h.astype(BF16), win_ref[...], preferred_element_type=F32)
    o_q = d_fourier
    o_k = o_q + d_diff
    o_v = o_k + d_diff
    o_p = o_v + d_diff

    ab_ref[...] = jnp.dot(proj[:, :o_q].astype(BF16), dftc_ref[...],
                          preferred_element_type=F32).astype(ab_ref.dtype)

    reps = d_diff // cos_ref.shape[1]
    cos = jnp.concatenate([cos_ref[...]] * reps, axis=1)
    sin = jnp.concatenate([sin_ref[...]] * reps, axis=1)
    lane = lax.broadcasted_iota(jnp.int32, (x.shape[0], d_diff), 1)
    first_half = (lane % head_dim) < (head_dim // 2)

    def rope(t):
        partner = jnp.where(first_half,
                            pltpu.roll(t, d_diff - head_dim // 2, 1),
                            pltpu.roll(t, head_dim // 2, 1))
        return t * cos + partner * sin

    q_ref[...] = (rope(proj[:, o_q:o_k]) * (head_dim ** -0.5)).astype(q_ref.dtype)
    k_ref[...] = rope(proj[:, o_k:o_v]).astype(k_ref.dtype)
    v_ref[...] = proj[:, o_v:o_p].astype(v_ref.dtype)
    up_ref[...] = proj[:, o_p:]


def _in_proj(x2, mod, g_pre, w_in, cos_t, sin_t, dft_chan, seq, d_fourier, d_diff, d_pool, head_dim):
    t, d = x2.shape
    tm = TOKEN_TILE
    tiles_per_seq = seq // tm
    tile = lambda n, dt: jax.ShapeDtypeStruct((t, n), dt)
    row = lambda n: pl.BlockSpec((tm, n), lambda i: (i, 0))
    return pl.pallas_call(
        functools.partial(_in_proj_kernel, d_fourier=d_fourier, d_diff=d_diff, head_dim=head_dim),
        out_shape=(tile(2 * d_fourier, BF16), tile(d_diff, BF16), tile(d_diff, BF16),
                   tile(d_diff, BF16), tile(d_pool, F32)),
        grid=(t // tm,),
        in_specs=[
            row(d),
            pl.BlockSpec((1, 3, d), lambda i: (i // tiles_per_seq, 0, 0)),
            _resident((1, d)),
            _resident(w_in.shape),
            pl.BlockSpec((tm, cos_t.shape[1]), lambda i: (i % tiles_per_seq, 0)),
            pl.BlockSpec((tm, sin_t.shape[1]), lambda i: (i % tiles_per_seq, 0)),
            _resident(dft_chan.shape),
        ],
        out_specs=(row(2 * d_fourier), row(d_diff), row(d_diff), row(d_diff), row(d_pool)),
        compiler_params=_params(1),
        name="in_proj",
    )(x2, mod, g_pre, w_in, cos_t, sin_t, dft_chan)


def _seq_dft_kernel(cs_ref, ss_ref, ab_ref, o_ref, *, d_fourier):
    a = ab_ref[:, :d_fourier]
    b = ab_ref[:, d_fourier:]
    re = (jnp.dot(cs_ref[...], a, preferred_element_type=F32)
          - jnp.dot(ss_ref[...], b, preferred_element_type=F32))
    o_ref[...] = re.astype(o_ref.dtype)


def _seq_dft(ab, dft_cos, dft_sin, batch, seq, d_fourier):
    tk = DFT_ROW_TILE
    n_row_tiles = seq // tk
    return pl.pallas_call(
        functools.partial(_seq_dft_kernel, d_fourier=d_fourier),
        out_shape=jax.ShapeDtypeStruct((batch * seq, d_fourier), BF16),
        grid=(n_row_tiles, batch),
        in_specs=[
            pl.BlockSpec((tk, seq), lambda i, b: (i, 0)),
            pl.BlockSpec((tk, seq), lambda i, b: (i, 0)),
            pl.BlockSpec((seq, 2 * d_fourier), lambda i, b: (b, 0)),
        ],
        out_specs=pl.BlockSpec((tk, d_fourier), lambda i, b: (b * n_row_tiles + i, 0)),
        compiler_params=_params(2),
        name="seq_dft",
    )(dft_cos, dft_sin, ab)


def _diff_attn_kernel(q_ref, k_ref, v_ref, lam_ref, gsub_ref, o_ref, *, head_dim, lambda_init):
    tq = q_ref.shape[0]
    q = q_ref[...]
    lane = lax.broadcasted_iota(jnp.int32, q.shape, 1)
    zero = jnp.zeros_like(q)
    q2 = jnp.concatenate([jnp.where(lane < head_dim, q, zero), jnp.where(lane >= head_dim, q, zero)], axis=0)
    s = lax.dot_general(q2, k_ref[...], (((1,), (1,)), ((), ())), preferred_element_type=F32)
    e = jnp.exp(s - jnp.max(s, axis=-1, keepdims=True))
    denom = jnp.sum(e, axis=-1, keepdims=True)
    o = jnp.dot(e.astype(BF16), v_ref[...], preferred_element_type=F32) / denom
    lq1k1 = jnp.sum(lam_ref[0:1, :] * lam_ref[1:2, :], axis=-1, keepdims=True)
    lq2k2 = jnp.sum(lam_ref[2:3, :] * lam_ref[3:4, :], axis=-1, keepdims=True)
    lam = jnp.exp(lq1k1) - jnp.exp(lq2k2) + lambda_init
    od = o[:tq] - lam * o[tq:]
    o_ref[...] = (_rms(od, SUBLN_EPS) * gsub_ref[...] * (1.0 - lambda_init)).astype(o_ref.dtype)


def _diff_attn(q, k, v, lam_vecs, g_subln, batch, seq, n_heads, head_dim, lambda_init):
    t, d_diff = q.shape
    dv = d_diff // n_heads
    tq = Q_TILE
    nq = seq // tq
    return pl.pallas_call(
        functools.partial(_diff_attn_kernel, head_dim=head_dim, lambda_init=lambda_init),
        out_shape=jax.ShapeDtypeStruct((t, d_diff), BF16),
        grid=(batch, n_heads, nq),
        in_specs=[
            pl.BlockSpec((tq, dv), lambda b, h, i: (b * nq + i, h)),
            pl.BlockSpec((seq, dv), lambda b, h, i: (b, h)),
            pl.BlockSpec((seq, dv), lambda b, h, i: (b, h)),
            _resident(lam_vecs.shape),
            _resident(g_subln.shape),
        ],
        out_specs=pl.BlockSpec((tq, dv), lambda b, h, i: (b * nq + i, h)),
        compiler_params=_params(3),
        name="diff_attn",
    )(q, k, v, lam_vecs, g_subln)


def _mix_out_kernel(x_ref, mod_ref, gpost_ref, f_ref, yd_ref, up_ref, prev_ref, next_ref,
                    wf_ref, wp_ref, pscale_ref, wout_ref, o_ref, *, seq, group_dim):
    tm, d_pool = up_ref.shape
    halo = prev_ref.shape[0]
    i = pl.program_id(0)
    tiles_per_seq = seq // tm
    j = i % tiles_per_seq

    y_f = jnp.dot(f_ref[...], wf_ref[...], preferred_element_type=F32)

    up = up_ref[...]
    prev = jnp.where(j == 0, 0.0, prev_ref[...])
    nxt = jnp.where(j == tiles_per_seq - 1, 0.0, next_ref[...])
    ext = jnp.concatenate([prev, up, nxt], axis=0)
    n = tm + 2 * halo
    sums = [ext + pltpu.roll(ext, 1, 0)]
    for w in POOL_WINDOWS[1:]:
        shift = w // 4
        sums.append(pltpu.roll(sums[-1], shift, 0) + pltpu.roll(sums[-1], n - shift, 0))
    lane_group = lax.broadcasted_iota(jnp.int32, (tm, d_pool), 1) // group_dim
    pos = j * tm + lax.broadcasted_iota(jnp.int32, (tm, d_pool), 0)
    win = jnp.zeros((tm, d_pool), F32)
    half = jnp.zeros((tm, d_pool), jnp.int32)
    for g, w in enumerate(POOL_WINDOWS):
        sel = lane_group == g
        win = jnp.where(sel, sums[g][halo:halo + tm], win)
        half = jnp.where(sel, w // 2, half)
    count = (jnp.minimum(pos + half, seq) - jnp.maximum(pos - half, 0)).astype(F32)
    pooled = win / count - up
    y_p = jnp.dot(pooled.astype(BF16), wp_ref[...], preferred_element_type=F32) * pscale_ref[...]

    cat = jnp.concatenate([y_f.astype(BF16), yd_ref[...], y_p.astype(BF16)], axis=1)
    y = jnp.dot(cat, wout_ref[...], preferred_element_type=F32)
    o_ref[...] = x_ref[...] + mod_ref[0][2:3] * (_rms(y, NORM_EPS) * gpost_ref[...])


def _mix_out(x2, mod, g_post, f_re, y_d, u_p, w_f, w_pool_bd, pool_scale, w_out, seq, group_dim):
    t, d = x2.shape
    tm = TOKEN_TILE
    halo = SUBLANES
    assert POOL_WINDOWS[0] == 2 and all(b == 2 * a for a, b in zip(POOL_WINDOWS, POOL_WINDOWS[1:]))
    assert halo >= max(POOL_WINDOWS) // 2
    tiles_per_seq = seq // tm
    blocks_per_tile = tm // halo
    last_halo_block = t // halo - 1
    row = lambda n: pl.BlockSpec((tm, n), lambda i: (i, 0))
    d_pool = u_p.shape[1]
    return pl.pallas_call(
        functools.partial(_mix_out_kernel, seq=seq, group_dim=group_dim),
        out_shape=jax.ShapeDtypeStruct((t, d), F32),
        grid=(t // tm,),
        in_specs=[
            row(d),
            pl.BlockSpec((1, 3, d), lambda i: (i // tiles_per_seq, 0, 0)),
            _resident((1, d)),
            row(f_re.shape[1]),
            row(y_d.shape[1]),
            row(d_pool),
            pl.BlockSpec((halo, d_pool), lambda i: (jnp.maximum(i * blocks_per_tile - 1, 0), 0)),
            pl.BlockSpec((halo, d_pool), lambda i: (jnp.minimum((i + 1) * blocks_per_tile, last_halo_block), 0)),
            _resident(w_f.shape),
            _resident(w_pool_bd.shape),
            _resident(pool_scale.shape),
            _resident(w_out.shape),
        ],
        out_specs=row(d),
        compiler_params=_params(1),
        name="mix_out",
    )(x2, mod, g_post, f_re, y_d, u_p, u_p, u_p, w_f, w_pool_bd, pool_scale, w_out)


def _rope_tables(positions, head_dim):
    inv = 1.0 / (ROPE_THETA ** (jnp.arange(0, head_dim, 2, dtype=F32) / head_dim))
    ang = positions.astype(F32)[:, None] * inv[None, :]
    cos = jnp.cos(ang)
    sin = jnp.sin(ang)
    reps = LANES // head_dim
    cos_t = jnp.tile(jnp.concatenate([cos, cos], axis=-1), (1, reps))
    sin_t = jnp.tile(jnp.concatenate([-sin, sin], axis=-1), (1, reps))
    return cos_t, sin_t


def _dft_tables(seq, n_heads, head_dim):
    k = jnp.arange(seq, dtype=jnp.int32)
    ang = (2.0 * math.pi / seq) * ((k[:, None] * k[None, :]) % seq).astype(F32)
    m = jnp.arange(head_dim, dtype=jnp.int32)
    ang_c = (2.0 * math.pi / head_dim) * ((m[:, None] * m[None, :]) % head_dim).astype(F32)
    norm = 1.0 / math.sqrt(seq * head_dim)
    eye = jnp.eye(n_heads, dtype=F32)
    chan = jnp.concatenate([jnp.kron(eye, jnp.cos(ang_c)), jnp.kron(eye, jnp.sin(ang_c))], axis=1) * norm
    return jnp.cos(ang).astype(BF16), jnp.sin(ang).astype(BF16), chan.astype(BF16)


def _block_diag(w):
    g, c, d = w.shape
    out = jnp.zeros((g * c, g * d), w.dtype)
    for i in range(g):
        out = out.at[i * c:(i + 1) * c, i * d:(i + 1) * d].set(w[i])
    return out


def kernel(x, c, positions, w_ada, b_ada, g_pre, g_post, w_ff_gu, w_ff_down, w_in, w_fourier,
           lambda_q1, lambda_k1, lambda_q2, lambda_k2, g_subln, w_pool, pool_scale, w_out):
    batch, seq, d = x.shape
    n_layers = w_ada.shape[0]
    d_fourier = w_fourier.shape[1]
    d_pool = pool_scale.shape[1]
    d_diff = (w_in.shape[2] - d_fourier - d_pool) // 3
    head_dim = lambda_q1.shape[1]
    group_dim = w_pool.shape[2]
    fourier_head_dim = d_fourier // N_FOURIER_HEADS
    assert seq % TOKEN_TILE == 0 and seq % Q_TILE == 0 and seq % DFT_ROW_TILE == 0
    assert d_diff == 2 * N_DIFF_HEADS * head_dim and len(POOL_WINDOWS) * group_dim == d_pool

    cos_t, sin_t = _rope_tables(positions, head_dim)
    dft_cos, dft_sin, dft_chan = _dft_tables(seq, N_FOURIER_HEADS, fourier_head_dim)

    ada = _ada(c, w_ada, b_ada).reshape(n_layers, batch, N_SUBLAYERS, 3, d)
    x2 = x.reshape(batch * seq, d)
    for l in range(n_layers):
        mods = [ada[l, :, s] for s in range(N_SUBLAYERS)]
        gp = lambda a, s: a[l, s].reshape(1, d)
        lambda_init = 0.8 - 0.6 * math.exp(-0.3 * l)

        x2 = _ffn(x2, mods[0], gp(g_pre, 0), gp(g_post, 0),
                  w_ff_gu[l, 0].astype(BF16), w_ff_down[l, 0].astype(BF16), seq)

        ab, q, k, v, u_p = _in_proj(x2, mods[1], gp(g_pre, 1), w_in[l].astype(BF16), cos_t, sin_t,
                                    dft_chan, seq, d_fourier, d_diff, d_pool, head_dim)
        f_re = _seq_dft(ab, dft_cos, dft_sin, batch, seq, d_fourier)
        lam_vecs = jnp.stack([lambda_q1[l], lambda_k1[l], lambda_q2[l], lambda_k2[l]]).astype(F32)
        y_d = _diff_attn(q, k, v, lam_vecs, g_subln[l].reshape(1, -1), batch, seq,
                         N_DIFF_HEADS, head_dim, lambda_init)
        x2 = _mix_out(x2, mods[1], gp(g_post, 1), f_re, y_d, u_p, w_fourier[l].astype(BF16),
                      _block_diag(w_pool[l]).astype(BF16), pool_scale[l].reshape(1, -1),
                      w_out[l].astype(BF16), seq, group_dim)

        x2 = _ffn(x2, mods[2], gp(g_pre, 2), gp(g_post, 2),
                  w_ff_gu[l, 1].astype(BF16), w_ff_down[l, 1].astype(BF16), seq)
    return x2.reshape(batch, seq, d)
```

```python
import functools
import math

import jax
import jax.numpy as jnp
from jax import lax
from jax.experimental import pallas as pl
from jax.experimental.pallas import tpu as pltpu

N_FOURIER_HEADS = 4
N_DIFF_HEADS = 4
POOL_WINDOWS = (2, 4, 8, 16)
N_SUBLAYERS = 3
ROPE_THETA = 10000.0
NORM_EPS = 1e-6
SUBLN_EPS = 1e-5
MACARON_WEIGHT = 0.5

LANES = 128
SUBLANES = 8
MXU_DEPTH = 256
VMEM_BYTES_V7X = 64 * 1024 * 1024
VMEM_LIMIT = VMEM_BYTES_V7X - 8 * 1024 * 1024

TOKEN_TILE = 512
Q_TILE = 512
KEY_CHUNK = 256
LOG2_E = math.log2(math.e)
DFT_ROW_TILE = 512
FF_CHUNKS_IN_MXU_TILES = (4, 4, 3)

BF16 = jnp.bfloat16
F32 = jnp.float32


def _rms(v, eps):
    return v * lax.rsqrt(jnp.mean(v * v, axis=-1, keepdims=True) + eps)


def _resident(shape):
    nd = len(shape)
    return pl.BlockSpec(shape, lambda *_: (0,) * nd, pipeline_mode=pl.Buffered(1))


def _params(n_axes):
    return pltpu.CompilerParams(dimension_semantics=("arbitrary",) * n_axes, vmem_limit_bytes=VMEM_LIMIT)


def _ada_kernel(c_ref, w_ref, b_ref, o_ref):
    c = c_ref[...]
    c_act = (c * jax.nn.sigmoid(c)).astype(BF16)
    o_ref[0] = jnp.dot(c_act, w_ref[0].astype(BF16), preferred_element_type=F32) + b_ref[0]


def _ada(c, w_ada, b_ada):
    n_layers, d, n = w_ada.shape
    b = c.shape[0]
    tn = n // 8
    return pl.pallas_call(
        _ada_kernel,
        out_shape=jax.ShapeDtypeStruct((n_layers, b, n), F32),
        grid=(n_layers, n // tn),
        in_specs=[
            pl.BlockSpec((b, d), lambda l, j: (0, 0)),
            pl.BlockSpec((1, d, tn), lambda l, j: (l, 0, j)),
            pl.BlockSpec((1, 1, tn), lambda l, j: (l, 0, j)),
        ],
        out_specs=pl.BlockSpec((1, b, tn), lambda l, j: (l, 0, j)),
        compiler_params=_params(2),
        name="ada",
    )(c, w_ada, b_ada.reshape(n_layers, 1, n))


def _ffn_kernel(x_ref, mod_ref, gpre_ref, gpost_ref, wgu_ref, wd_ref, o_ref, *, d_ff, chunks):
    x = x_ref[...]
    mod = mod_ref[0]
    h = _rms(x, NORM_EPS) * gpre_ref[...] * (1.0 + mod[1:2]) + mod[0:1]
    hb = h.astype(BF16)
    y = jnp.zeros(x.shape, F32)
    for c0, c1 in chunks:
        g = jnp.dot(hb, wgu_ref[:, c0:c1], preferred_element_type=F32)
        u = jnp.dot(hb, wgu_ref[:, d_ff + c0:d_ff + c1], preferred_element_type=F32)
        a = (g * jax.nn.sigmoid(g) * u).astype(BF16)
        y = y + jnp.dot(a, wd_ref[c0:c1, :], preferred_element_type=F32)
    o_ref[...] = x + MACARON_WEIGHT * mod[2:3] * (_rms(y, NORM_EPS) * gpost_ref[...])


def _ffn(x2, mod, g_pre, g_post, w_gu, w_down, seq):
    t, d = x2.shape
    d_ff = w_down.shape[0]
    tm = TOKEN_TILE
    tiles_per_seq = seq // tm
    bounds, c0 = [], 0
    for n_tiles in FF_CHUNKS_IN_MXU_TILES:
        bounds.append((c0, c0 + n_tiles * MXU_DEPTH))
        c0 += n_tiles * MXU_DEPTH
    assert c0 == d_ff, (c0, d_ff)
    return pl.pallas_call(
        functools.partial(_ffn_kernel, d_ff=d_ff, chunks=tuple(bounds)),
        out_shape=jax.ShapeDtypeStruct((t, d), F32),
        grid=(t // tm,),
        in_specs=[
            pl.BlockSpec((tm, d), lambda i: (i, 0)),
            pl.BlockSpec((1, 3, d), lambda i: (i // tiles_per_seq, 0, 0)),
            _resident((1, d)),
            _resident((1, d)),
            _resident(w_gu.shape),
            _resident(w_down.shape),
        ],
        out_specs=pl.BlockSpec((tm, d), lambda i: (i, 0)),
        compiler_params=_params(1),
        name="ffn",
    )(x2, mod, g_pre, g_post, w_gu, w_down)


def _in_proj_kernel(x_ref, mod_ref, gpre_ref, win_ref, cos_ref, sin_ref, dftc_ref,
                    ab_ref, q_ref, k_ref, v_ref, up_ref, *, d_fourier, d_diff, head_dim, n_heads):
    x = x_ref[...]
    mod = mod_ref[0]
    h = _rms(x, NORM_EPS) * gpre_ref[...] * (1.0 + mod[1:2]) + mod[0:1]
    proj = jnp.dot(h.astype(BF16), win_ref[...], preferred_element_type=F32)
    o_q = d_fourier
    o_k = o_q + d_diff
    o_v = o_k + d_diff
    o_p = o_v + d_diff

    ab_ref[...] = jnp.dot(proj[:, :o_q].astype(BF16), dftc_ref[...],
                          preferred_element_type=F32).astype(ab_ref.dtype)

    reps = d_diff // cos_ref.shape[1]
    cos = jnp.concatenate([cos_ref[...]] * reps, axis=1)
    sin = jnp.concatenate([sin_ref[...]] * reps, axis=1)
    lane = lax.broadcasted_iota(jnp.int32, (x.shape[0], d_diff), 1)
    first_half = (lane % head_dim) < (head_dim // 2)

    def rope(t):
        partner = jnp.where(first_half,
                            pltpu.roll(t, d_diff - head_dim // 2, 1),
                            pltpu.roll(t, head_dim // 2, 1))
        return t * cos + partner * sin

    q_ref[...] = (rope(proj[:, o_q:o_k]) * (head_dim ** -0.5 * LOG2_E)).astype(q_ref.dtype)
    k_ref[...] = rope(proj[:, o_k:o_v]).astype(k_ref.dtype)
    dv = d_diff // n_heads
    ones = jnp.ones((x.shape[0], dv), v_ref.dtype)
    v = proj[:, o_v:o_p].astype(v_ref.dtype)
    v_ref[...] = jnp.concatenate(
        [blk for hd in range(n_heads) for blk in (v[:, hd * dv:(hd + 1) * dv], ones)], axis=1)
    up_ref[...] = proj[:, o_p:]


def _in_proj(x2, mod, g_pre, w_in, cos_t, sin_t, dft_chan, seq, d_fourier, d_diff, d_pool, head_dim):
    t, d = x2.shape
    tm = TOKEN_TILE
    tiles_per_seq = seq // tm
    tile = lambda n, dt: jax.ShapeDtypeStruct((t, n), dt)
    row = lambda n: pl.BlockSpec((tm, n), lambda i: (i, 0))
    return pl.pallas_call(
        functools.partial(_in_proj_kernel, d_fourier=d_fourier, d_diff=d_diff, head_dim=head_dim,
                          n_heads=N_DIFF_HEADS),
        out_shape=(tile(2 * d_fourier, BF16), tile(d_diff, BF16), tile(d_diff, BF16),
                   tile(2 * d_diff, BF16), tile(d_pool, F32)),
        grid=(t // tm,),
        in_specs=[
            row(d),
            pl.BlockSpec((1, 3, d), lambda i: (i // tiles_per_seq, 0, 0)),
            _resident((1, d)),
            _resident(w_in.shape),
            pl.BlockSpec((tm, cos_t.shape[1]), lambda i: (i % tiles_per_seq, 0)),
            pl.BlockSpec((tm, sin_t.shape[1]), lambda i: (i % tiles_per_seq, 0)),
            _resident(dft_chan.shape),
        ],
        out_specs=(row(2 * d_fourier), row(d_diff), row(d_diff), row(2 * d_diff), row(d_pool)),
        compiler_params=_params(1),
        name="in_proj",
    )(x2, mod, g_pre, w_in, cos_t, sin_t, dft_chan)


def _seq_dft_kernel(cs_ref, ss_ref, ab_ref, o_ref, *, d_fourier):
    a = ab_ref[:, :d_fourier]
    b = ab_ref[:, d_fourier:]
    re = (jnp.dot(cs_ref[...], a, preferred_element_type=F32)
          - jnp.dot(ss_ref[...], b, preferred_element_type=F32))
    o_ref[...] = re.astype(o_ref.dtype)


def _seq_dft(ab, dft_cos, dft_sin, batch, seq, d_fourier):
    tk = DFT_ROW_TILE
    n_row_tiles = seq // tk
    return pl.pallas_call(
        functools.partial(_seq_dft_kernel, d_fourier=d_fourier),
        out_shape=jax.ShapeDtypeStruct((batch * seq, d_fourier), BF16),
        grid=(n_row_tiles, batch),
        in_specs=[
            pl.BlockSpec((tk, seq), lambda i, b: (i, 0)),
            pl.BlockSpec((tk, seq), lambda i, b: (i, 0)),
            pl.BlockSpec((seq, 2 * d_fourier), lambda i, b: (b, 0)),
        ],
        out_specs=pl.BlockSpec((tk, d_fourier), lambda i, b: (b * n_row_tiles + i, 0)),
        compiler_params=_params(2),
        name="seq_dft",
    )(dft_cos, dft_sin, ab)


def _diff_attn_kernel(q_ref, k_ref, v_ref, lam_ref, gsub_ref, o_ref, *, head_dim, lambda_init, key_chunk):
    tq, dv = q_ref.shape
    seq = k_ref.shape[0]
    q = q_ref[...]
    lane = lax.broadcasted_iota(jnp.int32, q.shape, 1)
    zero = jnp.zeros_like(q)
    q2 = jnp.concatenate([jnp.where(lane < head_dim, q, zero), jnp.where(lane >= head_dim, q, zero)], axis=0)

    m = acc = None
    for c0 in range(0, seq, key_chunk):
        s = lax.dot_general(q2, k_ref[c0:c0 + key_chunk, :], (((1,), (1,)), ((), ())),
                            preferred_element_type=F32)
        cm = jnp.max(s, axis=-1, keepdims=True)
        m_new = cm if m is None else jnp.maximum(m, cm)
        pv = jnp.dot(jnp.exp2(s - m_new).astype(BF16), v_ref[c0:c0 + key_chunk, :],
                     preferred_element_type=F32)
        acc = pv if acc is None else jnp.exp2(m - m_new) * acc + pv
        m = m_new
    o = acc[:, :dv] / acc[:, dv:]

    lq1k1 = jnp.sum(lam_ref[0:1, :] * lam_ref[1:2, :], axis=-1, keepdims=True)
    lq2k2 = jnp.sum(lam_ref[2:3, :] * lam_ref[3:4, :], axis=-1, keepdims=True)
    lam = jnp.exp(lq1k1) - jnp.exp(lq2k2) + lambda_init
    od = o[:tq] - lam * o[tq:]
    o_ref[...] = (_rms(od, SUBLN_EPS) * gsub_ref[...] * (1.0 - lambda_init)).astype(o_ref.dtype)


def _diff_attn(q, k, v_aug, lam_vecs, g_subln, batch, seq, n_heads, head_dim, lambda_init):
    t, d_diff = q.shape
    dv = d_diff // n_heads
    tq = Q_TILE
    nq = seq // tq
    return pl.pallas_call(
        functools.partial(_diff_attn_kernel, head_dim=head_dim, lambda_init=lambda_init,
                          key_chunk=KEY_CHUNK),
        out_shape=jax.ShapeDtypeStruct((t, d_diff), BF16),
        grid=(batch, n_heads, nq),
        in_specs=[
            pl.BlockSpec((tq, dv), lambda b, h, i: (b * nq + i, h)),
            pl.BlockSpec((seq, dv), lambda b, h, i: (b, h)),
            pl.BlockSpec((seq, 2 * dv), lambda b, h, i: (b, h)),
            _resident(lam_vecs.shape),
            _resident(g_subln.shape),
        ],
        out_specs=pl.BlockSpec((tq, dv), lambda b, h, i: (b * nq + i, h)),
        compiler_params=_params(3),
        name="diff_attn",
    )(q, k, v_aug, lam_vecs, g_subln)


def _mix_out_kernel(x_ref, mod_ref, gpost_ref, f_ref, yd_ref, up_ref, prev_ref, next_ref,
                    wf_ref, wp_ref, pscale_ref, wout_ref, o_ref, *, seq, group_dim):
    tm, d_pool = up_ref.shape
    halo = prev_ref.shape[0]
    i = pl.program_id(0)
    tiles_per_seq = seq // tm
    j = i % tiles_per_seq

    y_f = jnp.dot(f_ref[...], wf_ref[...], preferred_element_type=F32)

    up = up_ref[...]
    prev = jnp.where(j == 0, 0.0, prev_ref[...])
    nxt = jnp.where(j == tiles_per_seq - 1, 0.0, next_ref[...])
    ext = jnp.concatenate([prev, up, nxt], axis=0)
    n = tm + 2 * halo
    sums = [ext + pltpu.roll(ext, 1, 0)]
    for w in POOL_WINDOWS[1:]:
        shift = w // 4
        sums.append(pltpu.roll(sums[-1], shift, 0) + pltpu.roll(sums[-1], n - shift, 0))
    lane_group = lax.broadcasted_iota(jnp.int32, (tm, d_pool), 1) // group_dim
    pos = j * tm + lax.broadcasted_iota(jnp.int32, (tm, d_pool), 0)
    win = jnp.zeros((tm, d_pool), F32)
    half = jnp.zeros((tm, d_pool), jnp.int32)
    for g, w in enumerate(POOL_WINDOWS):
        sel = lane_group == g
        win = jnp.where(sel, sums[g][halo:halo + tm], win)
        half = jnp.where(sel, w // 2, half)
    count = (jnp.minimum(pos + half, seq) - jnp.maximum(pos - half, 0)).astype(F32)
    pooled = win / count - up
    y_p = jnp.dot(pooled.astype(BF16), wp_ref[...], preferred_element_type=F32) * pscale_ref[...]

    cat = jnp.concatenate([y_f.astype(BF16), yd_ref[...], y_p.astype(BF16)], axis=1)
    y = jnp.dot(cat, wout_ref[...], preferred_element_type=F32)
    o_ref[...] = x_ref[...] + mod_ref[0][2:3] * (_rms(y, NORM_EPS) * gpost_ref[...])


def _mix_out(x2, mod, g_post, f_re, y_d, u_p, w_f, w_pool_bd, pool_scale, w_out, seq, group_dim):
    t, d = x2.shape
    tm = TOKEN_TILE
    halo = SUBLANES
    assert POOL_WINDOWS[0] == 2 and all(b == 2 * a for a, b in zip(POOL_WINDOWS, POOL_WINDOWS[1:]))
    assert halo >= max(POOL_WINDOWS) // 2
    tiles_per_seq = seq // tm
    blocks_per_tile = tm // halo
    last_halo_block = t // halo - 1
    row = lambda n: pl.BlockSpec((tm, n), lambda i: (i, 0))
    d_pool = u_p.shape[1]
    return pl.pallas_call(
        functools.partial(_mix_out_kernel, seq=seq, group_dim=group_dim),
        out_shape=jax.ShapeDtypeStruct((t, d), F32),
        grid=(t // tm,),
        in_specs=[
            row(d),
            pl.BlockSpec((1, 3, d), lambda i: (i // tiles_per_seq, 0, 0)),
            _resident((1, d)),
            row(f_re.shape[1]),
            row(y_d.shape[1]),
            row(d_pool),
            pl.BlockSpec((halo, d_pool), lambda i: (jnp.maximum(i * blocks_per_tile - 1, 0), 0)),
            pl.BlockSpec((halo, d_pool), lambda i: (jnp.minimum((i + 1) * blocks_per_tile, last_halo_block), 0)),
            _resident(w_f.shape),
            _resident(w_pool_bd.shape),
            _resident(pool_scale.shape),
            _resident(w_out.shape),
        ],
        out_specs=row(d),
        compiler_params=_params(1),
        name="mix_out",
    )(x2, mod, g_post, f_re, y_d, u_p, u_p, u_p, w_f, w_pool_bd, pool_scale, w_out)


def _rope_tables(positions, head_dim):
    inv = 1.0 / (ROPE_THETA ** (jnp.arange(0, head_dim, 2, dtype=F32) / head_dim))
    ang = positions.astype(F32)[:, None] * inv[None, :]
    cos = jnp.cos(ang)
    sin = jnp.sin(ang)
    reps = LANES // head_dim
    cos_t = jnp.tile(jnp.concatenate([cos, cos], axis=-1), (1, reps))
    sin_t = jnp.tile(jnp.concatenate([-sin, sin], axis=-1), (1, reps))
    return cos_t, sin_t


def _dft_tables(seq, n_heads, head_dim):
    k = jnp.arange(seq, dtype=jnp.int32)
    ang = (2.0 * math.pi / seq) * ((k[:, None] * k[None, :]) % seq).astype(F32)
    m = jnp.arange(head_dim, dtype=jnp.int32)
    ang_c = (2.0 * math.pi / head_dim) * ((m[:, None] * m[None, :]) % head_dim).astype(F32)
    norm = 1.0 / math.sqrt(seq * head_dim)
    eye = jnp.eye(n_heads, dtype=F32)
    chan = jnp.concatenate([jnp.kron(eye, jnp.cos(ang_c)), jnp.kron(eye, jnp.sin(ang_c))], axis=1) * norm
    return jnp.cos(ang).astype(BF16), jnp.sin(ang).astype(BF16), chan.astype(BF16)


def _block_diag(w):
    g, c, d = w.shape
    out = jnp.zeros((g * c, g * d), w.dtype)
    for i in range(g):
        out = out.at[i * c:(i + 1) * c, i * d:(i + 1) * d].set(w[i])
    return out


def kernel(x, c, positions, w_ada, b_ada, g_pre, g_post, w_ff_gu, w_ff_down, w_in, w_fourier,
           lambda_q1, lambda_k1, lambda_q2, lambda_k2, g_subln, w_pool, pool_scale, w_out):
    batch, seq, d = x.shape
    n_layers = w_ada.shape[0]
    d_fourier = w_fourier.shape[1]
    d_pool = pool_scale.shape[1]
    d_diff = (w_in.shape[2] - d_fourier - d_pool) // 3
    head_dim = lambda_q1.shape[1]
    group_dim = w_pool.shape[2]
    fourier_head_dim = d_fourier // N_FOURIER_HEADS
    assert seq % TOKEN_TILE == 0 and seq % Q_TILE == 0 and seq % DFT_ROW_TILE == 0
    assert d_diff == 2 * N_DIFF_HEADS * head_dim and len(POOL_WINDOWS) * group_dim == d_pool

    cos_t, sin_t = _rope_tables(positions, head_dim)
    dft_cos, dft_sin, dft_chan = _dft_tables(seq, N_FOURIER_HEADS, fourier_head_dim)

    ada = _ada(c, w_ada, b_ada).reshape(n_layers, batch, N_SUBLAYERS, 3, d)
    x2 = x.reshape(batch * seq, d)
    for l in range(n_layers):
        mods = [ada[l, :, s] for s in range(N_SUBLAYERS)]
        gp = lambda a, s: a[l, s].reshape(1, d)
        lambda_init = 0.8 - 0.6 * math.exp(-0.3 * l)

        x2 = _ffn(x2, mods[0], gp(g_pre, 0), gp(g_post, 0),
                  w_ff_gu[l, 0].astype(BF16), w_ff_down[l, 0].astype(BF16), seq)

        ab, q, k, v, u_p = _in_proj(x2, mods[1], gp(g_pre, 1), w_in[l].astype(BF16), cos_t, sin_t,
                                    dft_chan, seq, d_fourier, d_diff, d_pool, head_dim)
        f_re = _seq_dft(ab, dft_cos, dft_sin, batch, seq, d_fourier)
        lam_vecs = jnp.stack([lambda_q1[l], lambda_k1[l], lambda_q2[l], lambda_k2[l]]).astype(F32)
        y_d = _diff_attn(q, k, v, lam_vecs, g_subln[l].reshape(1, -1), batch, seq,
                         N_DIFF_HEADS, head_dim, lambda_init)
        x2 = _mix_out(x2, mods[1], gp(g_post, 1), f_re, y_d, u_p, w_fourier[l].astype(BF16),
                      _block_diag(w_pool[l]).astype(BF16), pool_scale[l].reshape(1, -1),
                      w_out[l].astype(BF16), seq, group_dim)

        x2 = _ffn(x2, mods[2], gp(g_pre, 2), gp(g_post, 2),
                  w_ff_gu[l, 1].astype(BF16), w_ff_down[l, 1].astype(BF16), seq)
    return x2.reshape(batch, seq, d)
```

```python
import functools
import math

import jax
import jax.numpy as jnp
from jax import lax
from jax.experimental import pallas as pl
from jax.experimental.pallas import tpu as pltpu

N_FOURIER_HEADS = 4
N_DIFF_HEADS = 4
POOL_WINDOWS = (2, 4, 8, 16)
N_SUBLAYERS = 3
ROPE_THETA = 10000.0
NORM_EPS = 1e-6
SUBLN_EPS = 1e-5
MACARON_WEIGHT = 0.5

LANES = 128
SUBLANES = 8
MXU_DEPTH = 256
VMEM_BYTES_V7X = 64 * 1024 * 1024
VMEM_LIMIT = VMEM_BYTES_V7X - 8 * 1024 * 1024

TOKEN_TILE = 512
Q_TILE = 1024
KEY_CHUNK = 256
LOG2_E = math.log2(math.e)
DFT_ROW_TILE = 512
DFT_ROW_PAD = 16
FF_CHUNKS_IN_MXU_TILES = (4, 4, 3)

BF16 = jnp.bfloat16
F32 = jnp.float32


def _rms(v, eps):
    return v * lax.rsqrt(jnp.mean(v * v, axis=-1, keepdims=True) + eps)


def _resident(shape):
    nd = len(shape)
    return pl.BlockSpec(shape, lambda *_: (0,) * nd, pipeline_mode=pl.Buffered(1))


def _params(n_axes):
    return pltpu.CompilerParams(dimension_semantics=("arbitrary",) * n_axes, vmem_limit_bytes=VMEM_LIMIT)


def _ada_kernel(c_ref, w_ref, b_ref, o_ref):
    c = c_ref[...]
    c_act = (c * jax.nn.sigmoid(c)).astype(BF16)
    o_ref[0] = jnp.dot(c_act, w_ref[0].astype(BF16), preferred_element_type=F32) + b_ref[0]


def _ada(c, w_ada, b_ada):
    n_layers, d, n = w_ada.shape
    b = c.shape[0]
    tn = n // 8
    return pl.pallas_call(
        _ada_kernel,
        out_shape=jax.ShapeDtypeStruct((n_layers, b, n), F32),
        grid=(n_layers, n // tn),
        in_specs=[
            pl.BlockSpec((b, d), lambda l, j: (0, 0)),
            pl.BlockSpec((1, d, tn), lambda l, j: (l, 0, j)),
            pl.BlockSpec((1, 1, tn), lambda l, j: (l, 0, j)),
        ],
        out_specs=pl.BlockSpec((1, b, tn), lambda l, j: (l, 0, j)),
        compiler_params=_params(2),
        name="ada",
    )(c, w_ada, b_ada.reshape(n_layers, 1, n))


def _swiglu(hb, wgu_ref, wd_ref, d_ff, chunks):
    y = jnp.zeros((hb.shape[0], wd_ref.shape[1]), F32)
    for c0, c1 in chunks:
        g = jnp.dot(hb, wgu_ref[:, c0:c1], preferred_element_type=F32)
        u = jnp.dot(hb, wgu_ref[:, d_ff + c0:d_ff + c1], preferred_element_type=F32)
        a = (g * jax.nn.sigmoid(g) * u).astype(BF16)
        y = y + jnp.dot(a, wd_ref[c0:c1, :], preferred_element_type=F32)
    return y


def _ff_chunks(d_ff):
    bounds, c0 = [], 0
    for n_mxu in FF_CHUNKS_IN_MXU_TILES:
        bounds.append((c0, c0 + n_mxu * MXU_DEPTH))
        c0 += n_mxu * MXU_DEPTH
    assert c0 == d_ff, (c0, d_ff)
    return tuple(bounds)


def _ffn_kernel(x_ref, mod_ref, gpre_ref, gpost_ref, wgu_ref, wd_ref, o_ref, *, d_ff, chunks):
    x = x_ref[...]
    mod = mod_ref[0]
    h = _rms(x, NORM_EPS) * gpre_ref[...] * (1.0 + mod[1:2]) + mod[0:1]
    y = _swiglu(h.astype(BF16), wgu_ref, wd_ref, d_ff, chunks)
    o_ref[...] = x + MACARON_WEIGHT * mod[2:3] * (_rms(y, NORM_EPS) * gpost_ref[...])


def _ffn(x2, mod, g_pre, g_post, w_gu, w_down, seq):
    t, d = x2.shape
    d_ff = w_down.shape[0]
    tm = TOKEN_TILE
    tiles_per_seq = seq // tm
    return pl.pallas_call(
        functools.partial(_ffn_kernel, d_ff=d_ff, chunks=_ff_chunks(d_ff)),
        out_shape=jax.ShapeDtypeStruct((t, d), F32),
        grid=(t // tm,),
        in_specs=[
            pl.BlockSpec((tm, d), lambda i: (i, 0)),
            pl.BlockSpec((1, 3, d), lambda i: (i // tiles_per_seq, 0, 0)),
            _resident((1, d)),
            _resident((1, d)),
            _resident(w_gu.shape),
            _resident(w_down.shape),
        ],
        out_specs=pl.BlockSpec((tm, d), lambda i: (i, 0)),
        compiler_params=_params(1),
        name="ffn",
    )(x2, mod, g_pre, g_post, w_gu, w_down)


def _in_proj_kernel(x_ref, mod_ref, gpre_ref, win_ref, cos_ref, sin_ref, dftc_ref,
                    ab_ref, q_ref, k_ref, v_ref, up_ref, *, d_fourier, d_diff, head_dim, n_heads):
    x = x_ref[...]
    mod = mod_ref[0]
    h = _rms(x, NORM_EPS) * gpre_ref[...] * (1.0 + mod[1:2]) + mod[0:1]
    proj = jnp.dot(h.astype(BF16), win_ref[...], preferred_element_type=F32)
    o_q = d_fourier
    o_k = o_q + d_diff
    o_v = o_k + d_diff
    o_p = o_v + d_diff

    ab_ref[...] = jnp.dot(proj[:, :o_q].astype(BF16), dftc_ref[...],
                          preferred_element_type=F32).astype(ab_ref.dtype)

    reps = d_diff // cos_ref.shape[1]
    cos = jnp.concatenate([cos_ref[...]] * reps, axis=1)
    sin = jnp.concatenate([sin_ref[...]] * reps, axis=1)
    lane = lax.broadcasted_iota(jnp.int32, (x.shape[0], d_diff), 1)
    first_half = (lane % head_dim) < (head_dim // 2)

    def rope(t):
        partner = jnp.where(first_half,
                            pltpu.roll(t, d_diff - head_dim // 2, 1),
                            pltpu.roll(t, head_dim // 2, 1))
        return t * cos + partner * sin

    q_ref[...] = (rope(proj[:, o_q:o_k]) * (head_dim ** -0.5 * LOG2_E)).astype(q_ref.dtype)
    k_ref[...] = rope(proj[:, o_k:o_v]).astype(k_ref.dtype)
    dv = d_diff // n_heads
    ones = jnp.ones((x.shape[0], dv), v_ref.dtype)
    v = proj[:, o_v:o_p].astype(v_ref.dtype)
    v_ref[...] = jnp.concatenate(
        [blk for hd in range(n_heads) for blk in (v[:, hd * dv:(hd + 1) * dv], ones)], axis=1)
    up_ref[...] = proj[:, o_p:]


def _in_proj(x2, mod, g_pre, w_in, cos_t, sin_t, dft_chan, seq, d_fourier, d_diff, d_pool, head_dim):
    t, d = x2.shape
    tm = TOKEN_TILE
    tiles_per_seq = seq // tm
    tile = lambda n, dt: jax.ShapeDtypeStruct((t, n), dt)
    row = lambda n: pl.BlockSpec((tm, n), lambda i: (i, 0))
    return pl.pallas_call(
        functools.partial(_in_proj_kernel, d_fourier=d_fourier, d_diff=d_diff, head_dim=head_dim,
                          n_heads=N_DIFF_HEADS),
        out_shape=(tile(2 * d_fourier, BF16), tile(d_diff, BF16), tile(d_diff, BF16),
                   tile(2 * d_diff, BF16), tile(d_pool, F32)),
        grid=(t // tm,),
        in_specs=[
            row(d),
            pl.BlockSpec((1, 3, d), lambda i: (i // tiles_per_seq, 0, 0)),
            _resident((1, d)),
            _resident(w_in.shape),
            pl.BlockSpec((tm, cos_t.shape[1]), lambda i: (i % tiles_per_seq, 0)),
            pl.BlockSpec((tm, sin_t.shape[1]), lambda i: (i % tiles_per_seq, 0)),
            _resident(dft_chan.shape),
        ],
        out_specs=(row(2 * d_fourier), row(d_diff), row(d_diff), row(2 * d_diff), row(d_pool)),
        compiler_params=_params(1),
        name="in_proj",
    )(x2, mod, g_pre, w_in, cos_t, sin_t, dft_chan)


def _seq_dft_kernel(cs_ref, ss_ref, flip_ref, ab_ref, lo_ref, hi_ref, *, d_fourier):
    tk = lo_ref.shape[0]
    p = jnp.dot(cs_ref[...], ab_ref[:, :d_fourier], preferred_element_type=F32)
    q = jnp.dot(ss_ref[...], ab_ref[:, d_fourier:], preferred_element_type=F32)
    lo_ref[...] = (p - q)[:tk].astype(lo_ref.dtype)
    mirrored = p + q
    flipped = jnp.dot(flip_ref[...], mirrored[:tk].astype(BF16), preferred_element_type=F32)
    first = lax.broadcasted_iota(jnp.int32, flipped.shape, 0) == 0
    hi_ref[...] = jnp.where(first, mirrored[tk:tk + 1], flipped).astype(hi_ref.dtype)


def _seq_dft(ab, dft_cos, dft_sin, flip, batch, seq, d_fourier):
    tk = DFT_ROW_TILE
    half_tiles = seq // (2 * tk)
    rows = dft_cos.shape[0] // half_tiles
    out = jax.ShapeDtypeStruct((batch * half_tiles * tk, d_fourier), BF16)
    return pl.pallas_call(
        functools.partial(_seq_dft_kernel, d_fourier=d_fourier),
        out_shape=(out, out),
        grid=(half_tiles, batch),
        in_specs=[
            pl.BlockSpec((rows, seq), lambda i, b: (i, 0)),
            pl.BlockSpec((rows, seq), lambda i, b: (i, 0)),
            _resident(flip.shape),
            pl.BlockSpec((seq, 2 * d_fourier), lambda i, b: (b, 0)),
        ],
        out_specs=(pl.BlockSpec((tk, d_fourier), lambda i, b: (b * half_tiles + i, 0)),
                   pl.BlockSpec((tk, d_fourier), lambda i, b: (b * half_tiles + half_tiles - 1 - i, 0))),
        compiler_params=_params(2),
        name="seq_dft",
    )(dft_cos, dft_sin, flip, ab)


def _diff_attn_kernel(q_ref, k_ref, v_ref, lam_ref, gsub_ref, o_ref, *, head_dim, lambda_init, key_chunk):
    tq, dv = q_ref.shape
    seq = k_ref.shape[0]
    q = q_ref[...]
    lane = lax.broadcasted_iota(jnp.int32, q.shape, 1)
    zero = jnp.zeros_like(q)
    q2 = jnp.concatenate([jnp.where(lane < head_dim, q, zero), jnp.where(lane >= head_dim, q, zero)], axis=0)

    m = acc = None
    for c0 in range(0, seq, key_chunk):
        s = lax.dot_general(q2, k_ref[c0:c0 + key_chunk, :], (((1,), (1,)), ((), ())),
                            preferred_element_type=F32)
        cm = jnp.max(s, axis=-1, keepdims=True)
        m_new = cm if m is None else jnp.maximum(m, cm)
        pv = jnp.dot(jnp.exp2(s - m_new).astype(BF16), v_ref[c0:c0 + key_chunk, :],
                     preferred_element_type=F32)
        acc = pv if acc is None else jnp.exp2(m - m_new) * acc + pv
        m = m_new
    o = acc[:, :dv] / acc[:, dv:]

    lq1k1 = jnp.sum(lam_ref[0:1, :] * lam_ref[1:2, :], axis=-1, keepdims=True)
    lq2k2 = jnp.sum(lam_ref[2:3, :] * lam_ref[3:4, :], axis=-1, keepdims=True)
    lam = jnp.exp(lq1k1) - jnp.exp(lq2k2) + lambda_init
    od = o[:tq] - lam * o[tq:]
    o_ref[...] = (_rms(od, SUBLN_EPS) * gsub_ref[...] * (1.0 - lambda_init)).astype(o_ref.dtype)


def _diff_attn(q, k, v_aug, lam_vecs, g_subln, batch, seq, n_heads, head_dim, lambda_init):
    t, d_diff = q.shape
    dv = d_diff // n_heads
    tq = Q_TILE
    nq = seq // tq
    return pl.pallas_call(
        functools.partial(_diff_attn_kernel, head_dim=head_dim, lambda_init=lambda_init,
                          key_chunk=KEY_CHUNK),
        out_shape=jax.ShapeDtypeStruct((t, d_diff), BF16),
        grid=(batch, n_heads, nq),
        in_specs=[
            pl.BlockSpec((tq, dv), lambda b, h, i: (b * nq + i, h)),
            pl.BlockSpec((seq, dv), lambda b, h, i: (b, h)),
            pl.BlockSpec((seq, 2 * dv), lambda b, h, i: (b, h)),
            _resident(lam_vecs.shape),
            _resident(g_subln.shape),
        ],
        out_specs=pl.BlockSpec((tq, dv), lambda b, h, i: (b * nq + i, h)),
        compiler_params=_params(3),
        name="diff_attn",
    )(q, k, v_aug, lam_vecs, g_subln)


def _mix_out_kernel(x_ref, mod_ref, gpost_ref, flo_ref, fhi_ref, yd_ref, up_ref, prev_ref, next_ref,
                    wf_ref, wp_ref, pscale_ref, wout_ref, o_ref, *, seq, group_dim):
    tm, d_pool = up_ref.shape
    halo = prev_ref.shape[0]
    i = pl.program_id(0)
    tiles_per_seq = seq // tm
    j = i % tiles_per_seq

    f_re = jnp.where(j < tiles_per_seq // 2, flo_ref[...], fhi_ref[...])
    y_f = jnp.dot(f_re, wf_ref[...], preferred_element_type=F32)

    up = up_ref[...]
    prev = jnp.where(j == 0, 0.0, prev_ref[...])
    nxt = jnp.where(j == tiles_per_seq - 1, 0.0, next_ref[...])
    ext = jnp.concatenate([prev, up, nxt], axis=0)
    n = tm + 2 * halo
    sums = [ext + pltpu.roll(ext, 1, 0)]
    for w in POOL_WINDOWS[1:]:
        shift = w // 4
        sums.append(pltpu.roll(sums[-1], shift, 0) + pltpu.roll(sums[-1], n - shift, 0))
    lane_group = lax.broadcasted_iota(jnp.int32, (tm, d_pool), 1) // group_dim
    pos = j * tm + lax.broadcasted_iota(jnp.int32, (tm, d_pool), 0)
    win = jnp.zeros((tm, d_pool), F32)
    half = jnp.zeros((tm, d_pool), jnp.int32)
    for g, w in enumerate(POOL_WINDOWS):
        sel = lane_group == g
        win = jnp.where(sel, sums[g][halo:halo + tm], win)
        half = jnp.where(sel, w // 2, half)
    count = (jnp.minimum(pos + half, seq) - jnp.maximum(pos - half, 0)).astype(F32)
    pooled = win / count - up
    y_p = jnp.dot(pooled.astype(BF16), wp_ref[...], preferred_element_type=F32) * pscale_ref[...]

    cat = jnp.concatenate([y_f.astype(BF16), yd_ref[...], y_p.astype(BF16)], axis=1)
    y = jnp.dot(cat, wout_ref[...], preferred_element_type=F32)
    o_ref[...] = x_ref[...] + mod_ref[0][2:3] * (_rms(y, NORM_EPS) * gpost_ref[...])


def _mix_out(x2, mod, g_post, f_lo, f_hi, y_d, u_p, w_f, w_pool_bd, pool_scale, w_out, seq, group_dim):
    t, d = x2.shape
    tm = TOKEN_TILE
    halo = SUBLANES
    assert POOL_WINDOWS[0] == 2 and all(b == 2 * a for a, b in zip(POOL_WINDOWS, POOL_WINDOWS[1:]))
    assert halo >= max(POOL_WINDOWS) // 2
    tiles_per_seq = seq // tm
    half_tiles = tiles_per_seq // 2
    assert tm == DFT_ROW_TILE
    f_lo_tile = lambda i: (i // tiles_per_seq) * half_tiles + jnp.minimum(i % tiles_per_seq, half_tiles - 1)
    f_hi_tile = lambda i: (i // tiles_per_seq) * half_tiles + jnp.maximum(i % tiles_per_seq - half_tiles, 0)
    blocks_per_tile = tm // halo
    last_halo_block = t // halo - 1
    row = lambda n: pl.BlockSpec((tm, n), lambda i: (i, 0))
    d_pool = u_p.shape[1]
    return pl.pallas_call(
        functools.partial(_mix_out_kernel, seq=seq, group_dim=group_dim),
        out_shape=jax.ShapeDtypeStruct((t, d), F32),
        grid=(t // tm,),
        in_specs=[
            row(d),
            pl.BlockSpec((1, 3, d), lambda i: (i // tiles_per_seq, 0, 0)),
            _resident((1, d)),
            pl.BlockSpec((tm, f_lo.shape[1]), lambda i: (f_lo_tile(i), 0)),
            pl.BlockSpec((tm, f_hi.shape[1]), lambda i: (f_hi_tile(i), 0)),
            row(y_d.shape[1]),
            row(d_pool),
            pl.BlockSpec((halo, d_pool), lambda i: (jnp.maximum(i * blocks_per_tile - 1, 0), 0)),
            pl.BlockSpec((halo, d_pool), lambda i: (jnp.minimum((i + 1) * blocks_per_tile, last_halo_block), 0)),
            _resident(w_f.shape),
            _resident(w_pool_bd.shape),
            _resident(pool_scale.shape),
            _resident(w_out.shape),
        ],
        out_specs=row(d),
        compiler_params=_params(1),
        name="mix_out",
    )(x2, mod, g_post, f_lo, f_hi, y_d, u_p, u_p, u_p, w_f, w_pool_bd, pool_scale, w_out)


def _rope_tables(positions, head_dim):
    inv = 1.0 / (ROPE_THETA ** (jnp.arange(0, head_dim, 2, dtype=F32) / head_dim))
    ang = positions.astype(F32)[:, None] * inv[None, :]
    cos = jnp.cos(ang)
    sin = jnp.sin(ang)
    reps = LANES // head_dim
    cos_t = jnp.tile(jnp.concatenate([cos, cos], axis=-1), (1, reps))
    sin_t = jnp.tile(jnp.concatenate([-sin, sin], axis=-1), (1, reps))
    return cos_t, sin_t


def _dft_tables(seq, n_heads, head_dim):
    tk = DFT_ROW_TILE
    rows = tk + DFT_ROW_PAD
    half_tiles = seq // (2 * tk)
    r = jnp.arange(half_tiles * rows, dtype=jnp.int32)
    k = tk * (r // rows) + jnp.minimum(r % rows, tk)
    radix = math.isqrt(seq)
    assert radix * radix == seq
    digit = jnp.arange(radix, dtype=jnp.int32)
    ang_hi = (2.0 * math.pi / radix) * ((k[:, None] * digit[None, :]) % radix).astype(F32)
    ang_lo = (2.0 * math.pi / seq) * ((k[:, None] * digit[None, :]) % seq).astype(F32)
    c_hi, s_hi = jnp.cos(ang_hi)[:, :, None], jnp.sin(ang_hi)[:, :, None]
    c_lo, s_lo = jnp.cos(ang_lo)[:, None, :], jnp.sin(ang_lo)[:, None, :]
    dft_cos = (c_hi * c_lo - s_hi * s_lo).astype(BF16).reshape(half_tiles * rows, seq)
    dft_sin = (s_hi * c_lo + c_hi * s_lo).astype(BF16).reshape(half_tiles * rows, seq)

    m = jnp.arange(head_dim, dtype=jnp.int32)
    ang_c = (2.0 * math.pi / head_dim) * ((m[:, None] * m[None, :]) % head_dim).astype(F32)
    norm = 1.0 / math.sqrt(seq * head_dim)
    eye = jnp.eye(n_heads, dtype=F32)
    chan = jnp.concatenate([jnp.kron(eye, jnp.cos(ang_c)), jnp.kron(eye, jnp.sin(ang_c))], axis=1) * norm

    flip = (jnp.arange(tk, dtype=jnp.int32)[None, :] == tk - jnp.arange(tk, dtype=jnp.int32)[:, None])
    return dft_cos, dft_sin, chan.astype(BF16), flip.astype(BF16)


def _block_diag(w):
    g, c, d = w.shape
    out = jnp.zeros((g * c, g * d), w.dtype)
    for i in range(g):
        out = out.at[i * c:(i + 1) * c, i * d:(i + 1) * d].set(w[i])
    return out


def kernel(x, c, positions, w_ada, b_ada, g_pre, g_post, w_ff_gu, w_ff_down, w_in, w_fourier,
           lambda_q1, lambda_k1, lambda_q2, lambda_k2, g_subln, w_pool, pool_scale, w_out):
    batch, seq, d = x.shape
    n_layers = w_ada.shape[0]
    d_fourier = w_fourier.shape[1]
    d_pool = pool_scale.shape[1]
    d_diff = (w_in.shape[2] - d_fourier - d_pool) // 3
    head_dim = lambda_q1.shape[1]
    group_dim = w_pool.shape[2]
    fourier_head_dim = d_fourier // N_FOURIER_HEADS
    assert seq % TOKEN_TILE == 0 and seq % Q_TILE == 0 and seq % (2 * DFT_ROW_TILE) == 0
    assert d_diff == 2 * N_DIFF_HEADS * head_dim and len(POOL_WINDOWS) * group_dim == d_pool

    cos_t, sin_t = _rope_tables(positions, head_dim)
    dft_cos, dft_sin, dft_chan, dft_flip = _dft_tables(seq, N_FOURIER_HEADS, fourier_head_dim)

    ada = _ada(c, w_ada, b_ada).reshape(n_layers, batch, N_SUBLAYERS, 3, d)
    x2 = x.reshape(batch * seq, d)
    for l in range(n_layers):
        mods = [ada[l, :, s] for s in range(N_SUBLAYERS)]
        gp = lambda a, s: a[l, s].reshape(1, d)
        lambda_init = 0.8 - 0.6 * math.exp(-0.3 * l)

        x2 = _ffn(x2, mods[0], gp(g_pre, 0), gp(g_post, 0),
                  w_ff_gu[l, 0].astype(BF16), w_ff_down[l, 0].astype(BF16), seq)

        ab, q, k, v, u_p = _in_proj(x2, mods[1], gp(g_pre, 1), w_in[l].astype(BF16), cos_t, sin_t,
                                    dft_chan, seq, d_fourier, d_diff, d_pool, head_dim)
        f_lo, f_hi = _seq_dft(ab, dft_cos, dft_sin, dft_flip, batch, seq, d_fourier)
        lam_vecs = jnp.stack([lambda_q1[l], lambda_k1[l], lambda_q2[l], lambda_k2[l]]).astype(F32)
        y_d = _diff_attn(q, k, v, lam_vecs, g_subln[l].reshape(1, -1), batch, seq,
                         N_DIFF_HEADS, head_dim, lambda_init)
        x2 = _mix_out(x2, mods[1], gp(g_post, 1), f_lo, f_hi, y_d, u_p, w_fourier[l].astype(BF16),
                      _block_diag(w_pool[l]).astype(BF16), pool_scale[l].reshape(1, -1),
                      w_out[l].astype(BF16), seq, group_dim)

        x2 = _ffn(x2, mods[2], gp(g_pre, 2), gp(g_post, 2),
                  w_ff_gu[l, 1].astype(BF16), w_ff_down[l, 1].astype(BF16), seq)
    return x2.reshape(batch, seq, d)
```

```python
import functools
import math

import jax
import jax.numpy as jnp
from jax import lax
from jax.experimental import pallas as pl
from jax.experimental.pallas import tpu as pltpu

N_FOURIER_HEADS = 4
N_DIFF_HEADS = 4
POOL_WINDOWS = (2, 4, 8, 16)
N_SUBLAYERS = 3
ROPE_THETA = 10000.0
NORM_EPS = 1e-6
SUBLN_EPS = 1e-5
MACARON_WEIGHT = 0.5

LANES = 128
SUBLANES = 8
MXU_DEPTH = 256
VMEM_BYTES_V7X = 64 * 1024 * 1024
VMEM_LIMIT = VMEM_BYTES_V7X - 8 * 1024 * 1024

TOKEN_TILE = 512
Q_TILE = 2048
KEY_CHUNK = 256
LOG2_E = math.log2(math.e)
DFT_ROW_TILE = 512
DFT_ROW_PAD = 16
DFT_SEQS_PER_STEP = 2
FF_CHUNKS_IN_MXU_TILES = (4, 4, 3)
BF16 = jnp.bfloat16
F32 = jnp.float32


def _rms(v, eps):
    return v * lax.rsqrt(jnp.mean(v * v, axis=-1, keepdims=True) + eps)


def _resident(shape):
    nd = len(shape)
    return pl.BlockSpec(shape, lambda *_: (0,) * nd, pipeline_mode=pl.Buffered(1))


def _resident_slice(stack, lead):
    rest = stack.shape[len(lead):]
    index = tuple(lead) + (0,) * len(rest)
    return pl.BlockSpec((None,) * len(lead) + rest, lambda *_: index, pipeline_mode=pl.Buffered(1))


def _params(n_axes):
    return pltpu.CompilerParams(dimension_semantics=("arbitrary",) * n_axes, vmem_limit_bytes=VMEM_LIMIT)


def _ada_kernel(c_ref, w_ref, b_ref, o_ref):
    c = c_ref[...]
    c_act = (c * jax.nn.sigmoid(c)).astype(BF16)
    o_ref[0] = jnp.dot(c_act, w_ref[0].astype(BF16), preferred_element_type=F32) + b_ref[0]


def _ada(c, w_ada, b_ada):
    n_layers, d, n = w_ada.shape
    b = c.shape[0]
    tn = n // 8
    return pl.pallas_call(
        _ada_kernel,
        out_shape=jax.ShapeDtypeStruct((n_layers, b, n), F32),
        grid=(n_layers, n // tn),
        in_specs=[
            pl.BlockSpec((b, d), lambda l, j: (0, 0)),
            pl.BlockSpec((1, d, tn), lambda l, j: (l, 0, j)),
            pl.BlockSpec((1, 1, tn), lambda l, j: (l, 0, j)),
        ],
        out_specs=pl.BlockSpec((1, b, tn), lambda l, j: (l, 0, j)),
        compiler_params=_params(2),
        name="ada",
    )(c, w_ada, b_ada.reshape(n_layers, 1, n))


def _swiglu(hb, wgu_ref, wd_ref, d_ff, chunks):
    y = jnp.zeros((hb.shape[0], wd_ref.shape[1]), F32)
    for c0, c1 in chunks:
        g = jnp.dot(hb, wgu_ref[:, c0:c1], preferred_element_type=F32)
        u = jnp.dot(hb, wgu_ref[:, d_ff + c0:d_ff + c1], preferred_element_type=F32)
        a = (g * jax.nn.sigmoid(g) * u).astype(BF16)
        y = y + jnp.dot(a, wd_ref[c0:c1, :], preferred_element_type=F32)
    return y


def _ff_chunks(d_ff):
    bounds, c0 = [], 0
    for n_mxu in FF_CHUNKS_IN_MXU_TILES:
        bounds.append((c0, c0 + n_mxu * MXU_DEPTH))
        c0 += n_mxu * MXU_DEPTH
    assert c0 == d_ff, (c0, d_ff)
    return tuple(bounds)


def _ffn_kernel(x_ref, mod_ref, gpre_ref, gpost_ref, wgu_ref, wd_ref, o_ref, *, d_ff, chunks):
    x = x_ref[...]
    mod = mod_ref[0]
    h = _rms(x, NORM_EPS) * gpre_ref[...] * (1.0 + mod[1:2]) + mod[0:1]
    y = _swiglu(h.astype(BF16), wgu_ref, wd_ref, d_ff, chunks)
    o_ref[...] = x + MACARON_WEIGHT * mod[2:3] * (_rms(y, NORM_EPS) * gpost_ref[...])


def _ffn(x2, mod, g_pre, g_post, w_gu, w_down, lead, seq):
    t, d = x2.shape
    d_ff = w_down.shape[-2]
    tm = TOKEN_TILE
    tiles_per_seq = seq // tm
    return pl.pallas_call(
        functools.partial(_ffn_kernel, d_ff=d_ff, chunks=_ff_chunks(d_ff)),
        out_shape=jax.ShapeDtypeStruct((t, d), F32),
        grid=(t // tm,),
        in_specs=[
            pl.BlockSpec((tm, d), lambda i: (i, 0)),
            pl.BlockSpec((1, 3, d), lambda i: (i // tiles_per_seq, 0, 0)),
            _resident((1, d)),
            _resident((1, d)),
            _resident_slice(w_gu, lead),
            _resident_slice(w_down, lead),
        ],
        out_specs=pl.BlockSpec((tm, d), lambda i: (i, 0)),
        compiler_params=_params(1),
        name="ffn",
    )(x2, mod, g_pre, g_post, w_gu, w_down)


def _in_proj_kernel(x_ref, mod_ref, gpre_ref, win_ref, cos_ref, sin_ref, dftc_ref,
                    ab_ref, q_ref, k_ref, v_ref, up_ref, *, d_fourier, d_diff, head_dim, n_heads):
    x = x_ref[...]
    mod = mod_ref[0]
    h = _rms(x, NORM_EPS) * gpre_ref[...] * (1.0 + mod[1:2]) + mod[0:1]
    proj = jnp.dot(h.astype(BF16), win_ref[...], preferred_element_type=F32)
    o_q = d_fourier
    o_k = o_q + d_diff
    o_v = o_k + d_diff
    o_p = o_v + d_diff

    ab_ref[...] = jnp.dot(proj[:, :o_q].astype(BF16), dftc_ref[...],
                          preferred_element_type=F32).astype(ab_ref.dtype)

    reps = d_diff // cos_ref.shape[1]
    cos = jnp.concatenate([cos_ref[...]] * reps, axis=1)
    sin = jnp.concatenate([sin_ref[...]] * reps, axis=1)
    lane = lax.broadcasted_iota(jnp.int32, (x.shape[0], d_diff), 1)
    first_half = (lane % head_dim) < (head_dim // 2)

    def rope(t):
        partner = jnp.where(first_half,
                            pltpu.roll(t, d_diff - head_dim // 2, 1),
                            pltpu.roll(t, head_dim // 2, 1))
        return t * cos + partner * sin

    q_ref[...] = (rope(proj[:, o_q:o_k]) * (head_dim ** -0.5 * LOG2_E)).astype(q_ref.dtype)
    k_ref[...] = rope(proj[:, o_k:o_v]).astype(k_ref.dtype)
    dv = d_diff // n_heads
    ones = jnp.ones((x.shape[0], dv), v_ref.dtype)
    v = proj[:, o_v:o_p].astype(v_ref.dtype)
    v_ref[...] = jnp.concatenate(
        [blk for hd in range(n_heads) for blk in (v[:, hd * dv:(hd + 1) * dv], ones)], axis=1)
    up_ref[...] = proj[:, o_p:]


def _in_proj(x2, mod, g_pre, w_in, lead, cos_t, sin_t, dft_chan, seq, d_fourier, d_diff, d_pool, head_dim):
    t, d = x2.shape
    tm = TOKEN_TILE
    tiles_per_seq = seq // tm
    tile = lambda n, dt: jax.ShapeDtypeStruct((t, n), dt)
    row = lambda n: pl.BlockSpec((tm, n), lambda i: (i, 0))
    return pl.pallas_call(
        functools.partial(_in_proj_kernel, d_fourier=d_fourier, d_diff=d_diff, head_dim=head_dim,
                          n_heads=N_DIFF_HEADS),
        out_shape=(tile(2 * d_fourier, BF16), tile(d_diff, BF16), tile(d_diff, BF16),
                   tile(2 * d_diff, BF16), tile(d_pool, F32)),
        grid=(t // tm,),
        in_specs=[
            row(d),
            pl.BlockSpec((1, 3, d), lambda i: (i // tiles_per_seq, 0, 0)),
            _resident((1, d)),
            _resident_slice(w_in, lead),
            pl.BlockSpec((tm, cos_t.shape[1]), lambda i: (i % tiles_per_seq, 0)),
            pl.BlockSpec((tm, sin_t.shape[1]), lambda i: (i % tiles_per_seq, 0)),
            _resident(dft_chan.shape),
        ],
        out_specs=(row(2 * d_fourier), row(d_diff), row(d_diff), row(2 * d_diff), row(d_pool)),
        compiler_params=_params(1),
        name="in_proj",
    )(x2, mod, g_pre, w_in, cos_t, sin_t, dft_chan)


def _flip_rows(flip_ref, block, first_row):
    flipped = jnp.dot(flip_ref[...], block, preferred_element_type=F32)
    first = lax.broadcasted_iota(jnp.int32, flipped.shape, 0) == 0
    return jnp.where(first, first_row, flipped)


def _seq_dft_kernel(cs_ref, ss_ref, flip_ref, ab_ref, lo_ref, hi_ref, *, d_fourier, seq):
    n_seq = ab_ref.shape[0] // seq
    tk = flip_ref.shape[0]
    half = seq // 2
    half_tiles = half // tk
    rows = cs_ref.shape[0] // half_tiles
    zero_row = jnp.zeros((1, 2 * d_fourier), F32)

    a_fold, b_fold, a_mid = [], [], []
    for n in range(n_seq):
        base = n * seq
        tiles = []
        for t in range(half_tiles):
            direct = ab_ref[base + t * tk:base + (t + 1) * tk, :].astype(F32)
            m0 = base + seq - (t + 1) * tk
            first = zero_row if t == 0 else ab_ref[m0 + tk:m0 + tk + 1, :].astype(F32)
            mirror = _flip_rows(flip_ref, ab_ref[m0:m0 + tk, :], first)
            tiles.append(jnp.concatenate([direct[:, :d_fourier] + mirror[:, :d_fourier],
                                          direct[:, d_fourier:] - mirror[:, d_fourier:]], axis=1).astype(BF16))
        folded = jnp.concatenate(tiles, axis=0)
        a_fold.append(folded[:, :d_fourier])
        b_fold.append(folded[:, d_fourier:])
        a_mid.append(ab_ref[base + half:base + half + 1, :d_fourier].astype(F32))
    a_fold = jnp.concatenate(a_fold, axis=1)
    b_fold = jnp.concatenate(b_fold, axis=1)
    a_mid = jnp.concatenate(a_mid, axis=1)
    parity = lax.broadcasted_iota(jnp.int32, (rows, 1), 0) % 2
    sign = (1 - 2 * parity).astype(F32)

    for i in range(half_tiles):
        p = jnp.dot(cs_ref[i * rows:(i + 1) * rows, :], a_fold, preferred_element_type=F32) + sign * a_mid
        q = jnp.dot(ss_ref[i * rows:(i + 1) * rows, :], b_fold, preferred_element_type=F32)
        direct = (p - q)[:tk].astype(lo_ref.dtype)
        mirrored = p + q
        flipped = _flip_rows(flip_ref, mirrored[:tk].astype(BF16), mirrored[tk:tk + 1]).astype(hi_ref.dtype)
        j = half_tiles - 1 - i
        for n in range(n_seq):
            cols = slice(n * d_fourier, (n + 1) * d_fourier)
            lo_ref[n * half + i * tk:n * half + (i + 1) * tk, :] = direct[:, cols]
            hi_ref[n * half + j * tk:n * half + (j + 1) * tk, :] = flipped[:, cols]


def _seq_dft(ab, dft_cos, dft_sin, flip, batch, seq, d_fourier):
    half = seq // 2
    n_seq = DFT_SEQS_PER_STEP
    assert flip.shape[0] % 2 == 0
    assert batch % n_seq == 0
    out = jax.ShapeDtypeStruct((batch * half, d_fourier), BF16)
    return pl.pallas_call(
        functools.partial(_seq_dft_kernel, d_fourier=d_fourier, seq=seq),
        out_shape=(out, out),
        grid=(batch // n_seq,),
        in_specs=[
            _resident(dft_cos.shape),
            _resident(dft_sin.shape),
            _resident(flip.shape),
            pl.BlockSpec((n_seq * seq, 2 * d_fourier), lambda b: (b, 0)),
        ],
        out_specs=(pl.BlockSpec((n_seq * half, d_fourier), lambda b: (b, 0)),
                   pl.BlockSpec((n_seq * half, d_fourier), lambda b: (b, 0))),
        compiler_params=_params(1),
        name="seq_dft",
    )(dft_cos, dft_sin, flip, ab)


def _diff_attn_kernel(q_ref, k_ref, v_ref, lam_ref, gsub_ref, o_ref, *, head_dim, lambda_init, key_chunk):
    tq, dv = q_ref.shape
    seq = k_ref.shape[0]
    q = q_ref[...]
    lane = lax.broadcasted_iota(jnp.int32, q.shape, 1)
    zero = jnp.zeros_like(q)
    q2 = jnp.concatenate([jnp.where(lane < head_dim, q, zero), jnp.where(lane >= head_dim, q, zero)], axis=0)

    m = acc = None
    for c0 in range(0, seq, key_chunk):
        s = lax.dot_general(q2, k_ref[c0:c0 + key_chunk, :], (((1,), (1,)), ((), ())),
                            preferred_element_type=F32)
        cm = jnp.max(s, axis=-1, keepdims=True)
        m_new = cm if m is None else jnp.maximum(m, cm)
        pv = jnp.dot(jnp.exp2(s - m_new).astype(BF16), v_ref[c0:c0 + key_chunk, :],
                     preferred_element_type=F32)
        acc = pv if acc is None else jnp.exp2(m - m_new) * acc + pv
        m = m_new
    o = acc[:, :dv] / acc[:, dv:]

    lq1k1 = jnp.sum(lam_ref[0:1, :] * lam_ref[1:2, :], axis=-1, keepdims=True)
    lq2k2 = jnp.sum(lam_ref[2:3, :] * lam_ref[3:4, :], axis=-1, keepdims=True)
    lam = jnp.exp(lq1k1) - jnp.exp(lq2k2) + lambda_init
    od = o[:tq] - lam * o[tq:]
    o_ref[...] = (_rms(od, SUBLN_EPS) * gsub_ref[...] * (1.0 - lambda_init)).astype(o_ref.dtype)


def _diff_attn(q, k, v_aug, lam_vecs, g_subln, batch, seq, n_heads, head_dim, lambda_init):
    t, d_diff = q.shape
    dv = d_diff // n_heads
    tq = Q_TILE
    nq = seq // tq
    return pl.pallas_call(
        functools.partial(_diff_attn_kernel, head_dim=head_dim, lambda_init=lambda_init,
                          key_chunk=KEY_CHUNK),
        out_shape=jax.ShapeDtypeStruct((t, d_diff), BF16),
        grid=(batch, n_heads, nq),
        in_specs=[
            pl.BlockSpec((tq, dv), lambda b, h, i: (b * nq + i, h)),
            pl.BlockSpec((seq, dv), lambda b, h, i: (b, h)),
            pl.BlockSpec((seq, 2 * dv), lambda b, h, i: (b, h)),
            _resident(lam_vecs.shape),
            _resident(g_subln.shape),
        ],
        out_specs=pl.BlockSpec((tq, dv), lambda b, h, i: (b * nq + i, h)),
        compiler_params=_params(3),
        name="diff_attn",
    )(q, k, v_aug, lam_vecs, g_subln)


def _mix_out_kernel(x_ref, mod_ref, gpost_ref, flo_ref, fhi_ref, yd_ref, up_ref, prev_ref, next_ref,
                    wf_ref, wp_ref, pscale_ref, wout_ref, o_ref, *, seq, group_dim):
    tm, d_pool = up_ref.shape
    halo = prev_ref.shape[0]
    i = pl.program_id(0)
    tiles_per_seq = seq // tm
    j = i % tiles_per_seq

    f_re = jnp.where(j < tiles_per_seq // 2, flo_ref[...], fhi_ref[...])
    y_f = jnp.dot(f_re, wf_ref[...], preferred_element_type=F32)

    up = up_ref[...]
    prev = jnp.where(j == 0, 0.0, prev_ref[...])
    nxt = jnp.where(j == tiles_per_seq - 1, 0.0, next_ref[...])
    ext = jnp.concatenate([prev, up, nxt], axis=0)
    n = tm + 2 * halo
    sums = [ext + pltpu.roll(ext, 1, 0)]
    for w in POOL_WINDOWS[1:]:
        shift = w // 4
        sums.append(pltpu.roll(sums[-1], shift, 0) + pltpu.roll(sums[-1], n - shift, 0))
    lane_group = lax.broadcasted_iota(jnp.int32, (tm, d_pool), 1) // group_dim
    pos = j * tm + lax.broadcasted_iota(jnp.int32, (tm, d_pool), 0)
    win = jnp.zeros((tm, d_pool), F32)
    half = jnp.zeros((tm, d_pool), jnp.int32)
    for g, w in enumerate(POOL_WINDOWS):
        sel = lane_group == g
        win = jnp.where(sel, sums[g][halo:halo + tm], win)
        half = jnp.where(sel, w // 2, half)
    count = (jnp.minimum(pos + half, seq) - jnp.maximum(pos - half, 0)).astype(F32)
    pooled = win / count - up
    y_p = jnp.dot(pooled.astype(BF16), wp_ref[...], preferred_element_type=F32) * pscale_ref[...]

    cat = jnp.concatenate([y_f.astype(BF16), yd_ref[...], y_p.astype(BF16)], axis=1)
    y = jnp.dot(cat, wout_ref[...], preferred_element_type=F32)
    o_ref[...] = x_ref[...] + mod_ref[0][2:3] * (_rms(y, NORM_EPS) * gpost_ref[...])


def _mix_out(x2, mod, g_post, f_lo, f_hi, y_d, u_p, w_f, w_pool_bd, pool_scale, w_out, lead, seq, group_dim):
    t, d = x2.shape
    tm = TOKEN_TILE
    halo = SUBLANES
    assert POOL_WINDOWS[0] == 2 and all(b == 2 * a for a, b in zip(POOL_WINDOWS, POOL_WINDOWS[1:]))
    assert halo >= max(POOL_WINDOWS) // 2
    tiles_per_seq = seq // tm
    half_tiles = tiles_per_seq // 2
    assert tm == DFT_ROW_TILE
    f_lo_tile = lambda i: (i // tiles_per_seq) * half_tiles + jnp.minimum(i % tiles_per_seq, half_tiles - 1)
    f_hi_tile = lambda i: (i // tiles_per_seq) * half_tiles + jnp.maximum(i % tiles_per_seq - half_tiles, 0)
    blocks_per_tile = tm // halo
    last_halo_block = t // halo - 1
    row = lambda n: pl.BlockSpec((tm, n), lambda i: (i, 0))
    d_pool = u_p.shape[1]
    return pl.pallas_call(
        functools.partial(_mix_out_kernel, seq=seq, group_dim=group_dim),
        out_shape=jax.ShapeDtypeStruct((t, d), F32),
        grid=(t // tm,),
        in_specs=[
            row(d),
            pl.BlockSpec((1, 3, d), lambda i: (i // tiles_per_seq, 0, 0)),
            _resident((1, d)),
            pl.BlockSpec((tm, f_lo.shape[1]), lambda i: (f_lo_tile(i), 0)),
            pl.BlockSpec((tm, f_hi.shape[1]), lambda i: (f_hi_tile(i), 0)),
            row(y_d.shape[1]),
            row(d_pool),
            pl.BlockSpec((halo, d_pool), lambda i: (jnp.maximum(i * blocks_per_tile - 1, 0), 0)),
            pl.BlockSpec((halo, d_pool), lambda i: (jnp.minimum((i + 1) * blocks_per_tile, last_halo_block), 0)),
            _resident_slice(w_f, lead),
            _resident(w_pool_bd.shape),
            _resident(pool_scale.shape),
            _resident_slice(w_out, lead),
        ],
        out_specs=row(d),
        compiler_params=_params(1),
        name="mix_out",
    )(x2, mod, g_post, f_lo, f_hi, y_d, u_p, u_p, u_p, w_f, w_pool_bd, pool_scale, w_out)


def _rope_tables(positions, head_dim):
    inv = 1.0 / (ROPE_THETA ** (jnp.arange(0, head_dim, 2, dtype=F32) / head_dim))
    ang = positions.astype(F32)[:, None] * inv[None, :]
    cos = jnp.cos(ang)
    sin = jnp.sin(ang)
    reps = LANES // head_dim
    cos_t = jnp.tile(jnp.concatenate([cos, cos], axis=-1), (1, reps))
    sin_t = jnp.tile(jnp.concatenate([-sin, sin], axis=-1), (1, reps))
    return cos_t, sin_t


def _dft_tables(seq, n_heads, head_dim):
    tk = DFT_ROW_TILE
    rows = tk + DFT_ROW_PAD
    half_tiles = seq // (2 * tk)
    r = jnp.arange(half_tiles * rows, dtype=jnp.int32)
    k = tk * (r // rows) + jnp.minimum(r % rows, tk)
    lo_digits = math.isqrt(seq)
    assert lo_digits * lo_digits == seq and lo_digits % 2 == 0
    hi_digits = lo_digits // 2
    ang_hi = (2.0 * math.pi / lo_digits) * (
        (k[:, None] * jnp.arange(hi_digits, dtype=jnp.int32)[None, :]) % lo_digits).astype(F32)
    ang_lo = (2.0 * math.pi / seq) * (
        (k[:, None] * jnp.arange(lo_digits, dtype=jnp.int32)[None, :]) % seq).astype(F32)
    c_hi, s_hi = jnp.cos(ang_hi)[:, :, None], jnp.sin(ang_hi)[:, :, None]
    c_lo, s_lo = jnp.cos(ang_lo)[:, None, :], jnp.sin(ang_lo)[:, None, :]
    dft_cos = (c_hi * c_lo - s_hi * s_lo).astype(BF16).reshape(half_tiles * rows, seq // 2)
    dft_sin = (s_hi * c_lo + c_hi * s_lo).astype(BF16).reshape(half_tiles * rows, seq // 2)

    m = jnp.arange(head_dim, dtype=jnp.int32)
    ang_c = (2.0 * math.pi / head_dim) * ((m[:, None] * m[None, :]) % head_dim).astype(F32)
    norm = 1.0 / math.sqrt(seq * head_dim)
    eye = jnp.eye(n_heads, dtype=F32)
    chan = jnp.concatenate([jnp.kron(eye, jnp.cos(ang_c)), jnp.kron(eye, jnp.sin(ang_c))], axis=1) * norm

    flip = (jnp.arange(tk, dtype=jnp.int32)[None, :] == tk - jnp.arange(tk, dtype=jnp.int32)[:, None])
    return dft_cos, dft_sin, chan.astype(BF16), flip.astype(BF16)


def _block_diag(w):
    g, c, d = w.shape
    out = jnp.zeros((g * c, g * d), w.dtype)
    for i in range(g):
        out = out.at[i * c:(i + 1) * c, i * d:(i + 1) * d].set(w[i])
    return out


def kernel(x, c, positions, w_ada, b_ada, g_pre, g_post, w_ff_gu, w_ff_down, w_in, w_fourier,
           lambda_q1, lambda_k1, lambda_q2, lambda_k2, g_subln, w_pool, pool_scale, w_out):
    batch, seq, d = x.shape
    n_layers = w_ada.shape[0]
    d_fourier = w_fourier.shape[1]
    d_pool = pool_scale.shape[1]
    d_diff = (w_in.shape[2] - d_fourier - d_pool) // 3
    head_dim = lambda_q1.shape[1]
    group_dim = w_pool.shape[2]
    fourier_head_dim = d_fourier // N_FOURIER_HEADS
    assert seq % TOKEN_TILE == 0 and seq % Q_TILE == 0 and seq % (2 * DFT_ROW_TILE) == 0
    assert d_diff == 2 * N_DIFF_HEADS * head_dim and len(POOL_WINDOWS) * group_dim == d_pool

    cos_t, sin_t = _rope_tables(positions, head_dim)
    dft_cos, dft_sin, dft_chan, dft_flip = _dft_tables(seq, N_FOURIER_HEADS, fourier_head_dim)

    w_gu_b, w_down_b = w_ff_gu.astype(BF16), w_ff_down.astype(BF16)
    w_in_b, w_f_b, w_out_b = w_in.astype(BF16), w_fourier.astype(BF16), w_out.astype(BF16)

    ada = _ada(c, w_ada, b_ada).reshape(n_layers, batch, N_SUBLAYERS, 3, d)
    x2 = x.reshape(batch * seq, d)
    for l in range(n_layers):
        mods = [ada[l, :, s] for s in range(N_SUBLAYERS)]
        gp = lambda a, s: a[l, s].reshape(1, d)
        lambda_init = 0.8 - 0.6 * math.exp(-0.3 * l)

        x2 = _ffn(x2, mods[0], gp(g_pre, 0), gp(g_post, 0), w_gu_b, w_down_b, (l, 0), seq)

        ab, q, k, v, u_p = _in_proj(x2, mods[1], gp(g_pre, 1), w_in_b, (l,), cos_t, sin_t,
                                    dft_chan, seq, d_fourier, d_diff, d_pool, head_dim)
        f_lo, f_hi = _seq_dft(ab, dft_cos, dft_sin, dft_flip, batch, seq, d_fourier)
        lam_vecs = jnp.stack([lambda_q1[l], lambda_k1[l], lambda_q2[l], lambda_k2[l]]).astype(F32)
        y_d = _diff_attn(q, k, v, lam_vecs, g_subln[l].reshape(1, -1), batch, seq,
                         N_DIFF_HEADS, head_dim, lambda_init)
        x2 = _mix_out(x2, mods[1], gp(g_post, 1), f_lo, f_hi, y_d, u_p, w_f_b,
                      _block_diag(w_pool[l]).astype(BF16), pool_scale[l].reshape(1, -1),
                      w_out_b, (l,), seq, group_dim)

        x2 = _ffn(x2, mods[2], gp(g_pre, 2), gp(g_post, 2), w_gu_b, w_down_b, (l, 1), seq)
    return x2.reshape(batch, seq, d)
```
